```python
import math, functools
import jax, jax.numpy as jnp
from jax import lax
import numpy as np

D_MODEL = 1024
BATCH = 4
SEQ = 4096
DEPTH = 1

N_META = 16
MIX_WIDTH = D_MODEL
MIX_A = MIX_WIDTH // 2
MIX_B = MIX_WIDTH - MIX_A
N_POOL_GROUPS = 4
POOL_GROUP_DIM = MIX_A // N_POOL_GROUPS
POOL_WINDOWS = (2, 4, 8, 16)
N_CONV_HEADS = 8
CONV_WIDTH = 3
MIX_IN = MIX_A + 3 * MIX_B
N_PEER_HEADS = 8
N_KEYS = 128
N_EXPERTS = N_KEYS * N_KEYS
D_QUERY = 256
D_QUERY_HALF = D_QUERY // 2
PEER_TOPK = 16
TOKEN_BLOCK = 128
EPS = 1e-6

kernel_name = "hymba_pool_shortconv_peer_layer"


def rmsnorm(x, g):
    xf = x.astype(jnp.float32)
    y = xf * lax.rsqrt(jnp.mean(xf * xf, axis=-1, keepdims=True) + EPS)
    return (y * g.astype(jnp.float32)).astype(x.dtype)


def causal_pool_minus_self(u, window):
    L = u.shape[1]
    uf = u.astype(jnp.float32)
    c = jnp.cumsum(uf, axis=1)
    c_shift = jnp.pad(c, ((0, 0), (window, 0), (0, 0)))[:, :L]
    count = jnp.minimum(jnp.arange(1, L + 1), window).astype(jnp.float32)
    mean = (c - c_shift) / count[None, :, None]
    return (mean - uf).astype(u.dtype)


def pool_mixer(a, pool_w, pool_scale):
    B, L, _ = a.shape
    groups = a.reshape(B, L, N_POOL_GROUPS, POOL_GROUP_DIM)
    pooled = jnp.stack([causal_pool_minus_self(groups[:, :, g], POOL_WINDOWS[g])
                        for g in range(N_POOL_GROUPS)], axis=2)
    y = jnp.einsum('blgc,gcd->blgd', pooled, pool_w)
    return y.reshape(B, L, MIX_A) * pool_scale


def short_conv_mixer(b_gate, c_gate, val, conv_w):
    u = c_gate * val
    L = u.shape[1]
    up = jnp.pad(u, ((0, 0), (CONV_WIDTH - 1, 0), (0, 0)))
    conv = up[:, 0:L] * conv_w[0]
    for k in range(1, CONV_WIDTH):
        conv = conv + up[:, k:k + L] * conv_w[k]
    return b_gate * conv


def peer_retrieve(xn, w_query, sub_keys):
    T = xn.shape[0]
    q = (xn @ w_query).reshape(T, N_PEER_HEADS, 2, D_QUERY_HALF)
    s = jnp.einsum('thpc,hpkc->thpk', q.astype(jnp.float32), sub_keys.astype(jnp.float32))
    s1, i1 = lax.top_k(s[:, :, 0], PEER_TOPK)
    s2, i2 = lax.top_k(s[:, :, 1], PEER_TOPK)
    cand_s = (s1[..., :, None] + s2[..., None, :]).reshape(T, N_PEER_HEADS, PEER_TOPK * PEER_TOPK)
    cand_i = (i1[..., :, None] * N_KEYS + i2[..., None, :]).reshape(T, N_PEER_HEADS, PEER_TOPK * PEER_TOPK)
    top_s, pos = lax.top_k(cand_s, PEER_TOPK)
    idx = jnp.take_along_axis(cand_i, pos, axis=-1)
    gates = jax.nn.softmax(top_s, axis=-1)
    return idx, gates


def peer_experts(xn, idx, gates, expert_u, expert_v):
    T, D = xn.shape
    HK = N_PEER_HEADS * PEER_TOPK
    pad = (-T) % TOKEN_BLOCK
    xb = jnp.pad(xn, ((0, pad), (0, 0))).reshape(-1, TOKEN_BLOCK, D)
    ib = jnp.pad(idx.reshape(T, HK), ((0, pad), (0, 0))).reshape(-1, TOKEN_BLOCK, HK)
    gb = jnp.pad(gates.reshape(T, HK), ((0, pad), (0, 0))).reshape(-1, TOKEN_BLOCK, HK)

    def block(args):
        xblk, iblk, gblk = args
        u = expert_u[iblk]
        v = expert_v[iblk]
        act = jax.nn.gelu(jnp.einsum('tkd,td->tk', u, xblk).astype(jnp.float32), approximate=False)
        return jnp.einsum('tk,tkd->td', (gblk * act).astype(v.dtype), v)

    out = lax.map(block, (xb, ib, gb))
    return out.reshape(-1, D)[:T].astype(xn.dtype)


def setup_inputs(seed: int = 0) -> dict:
    key = jax.random.key(seed)
    ks = jax.random.split(key, 16)
    f32 = jnp.float32
    nrm = lambda k, shape, scale: jax.random.normal(k, shape, f32) * scale
    gain = lambda k, shape: 1.0 + 0.02 * jax.random.normal(k, shape, f32)
    return {
        "x": jax.random.normal(ks[0], (BATCH, SEQ, D_MODEL), f32),
        "meta_tokens": nrm(ks[1], (N_META, D_MODEL), 1.0),
        "norm_mix_g": gain(ks[2], (DEPTH, D_MODEL)),
        "w_in": nrm(ks[3], (DEPTH, D_MODEL, MIX_IN), D_MODEL ** -0.5),
        "pool_w": nrm(ks[4], (DEPTH, N_POOL_GROUPS, POOL_GROUP_DIM, POOL_GROUP_DIM), POOL_GROUP_DIM ** -0.5),
        "pool_scale": 1.0 + 0.1 * jax.random.normal(ks[5], (DEPTH, MIX_A), f32),
        "conv_w": nrm(ks[6], (DEPTH, CONV_WIDTH, MIX_B), CONV_WIDTH ** -0.5),
        "norm_a_g": gain(ks[7], (DEPTH, MIX_A)),
        "norm_b_g": gain(ks[8], (DEPTH, MIX_B)),
        "w_out": nrm(ks[9], (DEPTH, MIX_WIDTH, D_MODEL), MIX_WIDTH ** -0.5),
        "norm_ffn_g": gain(ks[10], (DEPTH, D_MODEL)),
        "w_query": nrm(ks[11], (DEPTH, D_MODEL, N_PEER_HEADS * D_QUERY), D_MODEL ** -0.5),
        "sub_keys": nrm(ks[12], (DEPTH, N_PEER_HEADS, 2, N_KEYS, D_QUERY_HALF), D_QUERY_HALF ** -0.5),
        "expert_u": nrm(ks[13], (DEPTH, N_EXPERTS, D_MODEL), D_MODEL ** -0.5),
        "expert_v": nrm(ks[14], (DEPTH, N_EXPERTS, D_MODEL), 0.5),
        "norm_final_g": gain(ks[15], (D_MODEL,)),
    }


def reference(x, meta_tokens, norm_mix_g, w_in, pool_w, pool_scale, conv_w, norm_a_g, norm_b_g,
              w_out, norm_ffn_g, w_query, sub_keys, expert_u, expert_v, norm_final_g):
    B = x.shape[0]
    meta = jnp.broadcast_to(meta_tokens.astype(x.dtype)[None], (B, N_META, D_MODEL))
    h = jnp.concatenate([meta, x], axis=1)
    for layer in range(DEPTH):
        xn = rmsnorm(h, norm_mix_g[layer])
        proj = xn @ w_in[layer]
        a = proj[..., :MIX_A]
        b_gate = proj[..., MIX_A:MIX_A + MIX_B]
        c_gate = proj[..., MIX_A + MIX_B:MIX_A + 2 * MIX_B]
        val = proj[..., MIX_A + 2 * MIX_B:]
        ya = rmsnorm(pool_mixer(a, pool_w[layer], pool_scale[layer]), norm_a_g[layer])
        yb = rmsnorm(short_conv_mixer(b_gate, c_gate, val, conv_w[layer]), norm_b_g[layer])
        h = h + jnp.concatenate([ya, yb], axis=-1) @ w_out[layer]
        if layer == DEPTH - 1:
            h = h[:, N_META:]
        Bh, Lh, _ = h.shape
        xf = rmsnorm(h, norm_ffn_g[layer]).reshape(Bh * Lh, D_MODEL)
        idx, gates = peer_retrieve(xf, w_query[layer], sub_keys[layer])
        h = h + peer_experts(xf, idx, gates, expert_u[layer], expert_v[layer]).reshape(Bh, Lh, D_MODEL)
    return rmsnorm(h, norm_final_g)
```

```python
import functools
import math

import jax
import jax.numpy as jnp
from jax import lax
from jax.experimental import pallas as pl
from jax.experimental.pallas import tpu as pltpu

F32 = jnp.float32
BF16 = jnp.bfloat16

D_MODEL = 1024
N_META = 16
MIX_A = 512
MIX_B = 512
N_POOL_GROUPS = 4
POOL_GROUP_DIM = 128
POOL_WINDOWS = (2, 4, 8, 16)
CONV_WIDTH = 3
N_HEADS = 8
N_KEYS = 128
D_QUERY_HALF = 128
TOPK = 16
EPS = 1e-6

LANES = 128
SUBLANES = 8
HIST_A = 16
HIST_U = 8

SEQ_TILE = 512
TOK_TILE_R = 512
TOK_TILE_E = 512
EXP_TILE = 2048
EXP_CHUNK = 512
ROW_BLK = 16

VMEM_LIMIT = 56 * 1024 * 1024


def _rmsnorm(x, g):
    ms = jnp.mean(x * x, axis=-1, keepdims=True)
    return (x * lax.rsqrt(ms + EPS)) * g


def _mixer_kernel(x_ref, meta_ref, gmix_ref, win_ref, poolw_ref, pscale_ref, convw_ref,
                  ga_ref, gb_ref, wout_ref, h_ref, abuf, ubuf):
    j = pl.program_id(1)
    ts = x_ref.shape[1]

    def project(rows):
        xn = _rmsnorm(rows, gmix_ref[...])
        return jnp.dot(xn.astype(BF16), win_ref[...], preferred_element_type=F32)

    @pl.when(j == 0)
    def _():
        pm = project(meta_ref[...])
        abuf[0:HIST_A, :] = pm[:, 0:MIX_A]
        um = pm[:, MIX_A + MIX_B:MIX_A + 2 * MIX_B] * pm[:, MIX_A + 2 * MIX_B:]
        ubuf[0:HIST_U, :] = um[N_META - HIST_U:, :]

    @pl.when(j > 0)
    def _():
        abuf[0:HIST_A, :] = abuf[ts:ts + HIST_A, :]
        ubuf[0:HIST_U, :] = ubuf[ts:ts + HIST_U, :]

    x = x_ref[0]
    p = project(x)
    abuf[HIST_A:, :] = p[:, 0:MIX_A]
    ubuf[HIST_U:, :] = p[:, MIX_A + MIX_B:MIX_A + 2 * MIX_B] * p[:, MIX_A + 2 * MIX_B:]
    b_gate = p[:, MIX_A:MIX_A + MIX_B]

    ya_parts = []
    for g in range(N_POOL_GROUPS):
        w = POOL_WINDOWS[g]
        cols = slice(g * POOL_GROUP_DIM, (g + 1) * POOL_GROUP_DIM)
        a_self = abuf[HIST_A:HIST_A + ts, cols]
        acc = a_self
        for k in range(1, w):
            acc = acc + abuf[HIST_A - k:HIST_A - k + ts, cols]
        pooled = acc * (1.0 / w) - a_self
        y = jnp.dot(pooled.astype(BF16), poolw_ref[g], preferred_element_type=F32)
        ya_parts.append(y * pscale_ref[:, cols])
    ya = _rmsnorm(jnp.concatenate(ya_parts, axis=-1), ga_ref[...])

    conv = ubuf[HIST_U - 2:HIST_U - 2 + ts, :] * convw_ref[0:1, :]
    conv = conv + ubuf[HIST_U - 1:HIST_U - 1 + ts, :] * convw_ref[1:2, :]
    conv = conv + ubuf[HIST_U:HIST_U + ts, :] * convw_ref[2:3, :]
    yb = _rmsnorm(b_gate * conv, gb_ref[...])

    mixed = jnp.dot(ya.astype(BF16), wout_ref[0:MIX_A, :], preferred_element_type=F32)
    mixed = mixed + jnp.dot(yb.astype(BF16), wout_ref[MIX_A:, :], preferred_element_type=F32)
    h_ref[0] = x + mixed


def _mixer(x, meta, gmix, win, poolw, pscale, convw, ga, gb, wout):
    b, s, d = x.shape
    n_tiles = s // SEQ_TILE
    const = lambda *shape: pl.BlockSpec(shape, lambda bi, j: (0,) * len(shape))
    return pl.pallas_call(
        _mixer_kernel,
        grid=(b, n_tiles),
        in_specs=[
            pl.BlockSpec((1, SEQ_TILE, d), lambda bi, j: (bi, j, 0)),
            const(N_META, d), const(1, d), const(d, MIX_A + 3 * MIX_B),
            const(N_POOL_GROUPS, POOL_GROUP_DIM, POOL_GROUP_DIM), const(1, MIX_A),
            const(CONV_WIDTH, MIX_B), const(1, MIX_A), const(1, MIX_B), const(MIX_A + MIX_B, d),
        ],
        out_specs=pl.BlockSpec((1, SEQ_TILE, d), lambda bi, j: (bi, j, 0)),
        out_shape=jax.ShapeDtypeStruct((b, s, d), F32),
        scratch_shapes=[pltpu.VMEM((SEQ_TILE + HIST_A, MIX_A), F32),
                        pltpu.VMEM((SEQ_TILE + HIST_U, MIX_B), F32)],
        compiler_params=pltpu.CompilerParams(
            dimension_semantics=("arbitrary", "arbitrary"), vmem_limit_bytes=VMEM_LIMIT),
        name="mixer",
    )(x, meta, gmix, win, poolw, pscale, convw, ga, gb, wout)


def _top_values(x, k):
    out = []
    for r in range(k):
        m = jnp.max(x, axis=0, keepdims=True)
        out.append(m)
        if r + 1 < k:
            x = jnp.where(x == m, -jnp.inf, x)
    return out


def _retrieve_kernel(h_ref, gffn_ref, wq_ref, keys_ref, xfb_ref, s1_ref, a_ref, s2_ref, e2_ref,
                     tau_ref, xf_scr):
    hd = pl.program_id(1)
    tm = h_ref.shape[0]

    @pl.when(hd == 0)
    def _():
        xfb = _rmsnorm(h_ref[...], gffn_ref[...]).astype(BF16)
        xf_scr[...] = xfb
        xfb_ref[...] = xfb

    q = jnp.dot(xf_scr[...], wq_ref[...], preferred_element_type=F32).astype(BF16)
    nt = (((1,), (1,)), ((), ()))
    s1 = lax.dot_general(keys_ref[0, 0], q[:, 0:D_QUERY_HALF], nt, preferred_element_type=F32)
    s2 = lax.dot_general(keys_ref[0, 1], q[:, D_QUERY_HALF:], nt, preferred_element_type=F32)
    s1_ref[0] = s1
    s2_ref[0] = s2

    half = TOPK // 2
    for lt in range(tm // LANES):
        ls = slice(lt * LANES, (lt + 1) * LANES)
        x1 = s1[:, ls]
        x2 = s2[:, ls]
        t1 = _top_values(x1, TOPK)
        t2 = _top_values(x2, TOPK)
        t2_lo = jnp.concatenate(t2[:half], axis=0)
        t2_hi = jnp.concatenate(t2[half:], axis=0)
        cands = [t1[0] + t2_lo, t1[0] + t2_hi] + [t1[i] + t2_lo for i in range(1, TOPK)]
        cs = _top_values(jnp.concatenate(cands, axis=0), TOPK)
        m = cs[0]
        z = jnp.ones_like(m)
        for c in cs[1:]:
            z = z + jnp.exp(c - m)
        rz = 1.0 / z
        tau_ref[0, :, ls] = cs[TOPK - 1]
        a_ref[0, :, ls] = jnp.exp(x1 - t1[0]) * rz
        e2_ref[0, :, ls] = jnp.exp(x2 - t2[0])


def _retrieve(h2d, gffn, wq, keys):
    t, d = h2d.shape
    n_tiles = t // TOK_TILE_R
    per_head = lambda: pl.BlockSpec((1, N_KEYS, TOK_TILE_R), lambda i, hd: (hd, 0, i))
    stat = jax.ShapeDtypeStruct((N_HEADS, N_KEYS, t), F32)
    return pl.pallas_call(
        _retrieve_kernel,
        grid=(n_tiles, N_HEADS),
        in_specs=[
            pl.BlockSpec((TOK_TILE_R, d), lambda i, hd: (i, 0)),
            pl.BlockSpec((1, d), lambda i, hd: (0, 0)),
            pl.BlockSpec((d, 2 * D_QUERY_HALF), lambda i, hd: (0, hd)),
            pl.BlockSpec((1, 2, N_KEYS, D_QUERY_HALF), lambda i, hd: (hd, 0, 0, 0)),
        ],
        out_specs=[
            pl.BlockSpec((TOK_TILE_R, d), lambda i, hd: (i, 0)),
            per_head(), per_head(), per_head(), per_head(),
            pl.BlockSpec((1, 1, TOK_TILE_R), lambda i, hd: (hd, 0, i)),
        ],
        out_shape=[jax.ShapeDtypeStruct((t, d), BF16), stat, stat, stat, stat,
                   jax.ShapeDtypeStruct((N_HEADS, 1, t), F32)],
        scratch_shapes=[pltpu.VMEM((TOK_TILE_R, d), BF16)],
        compiler_params=pltpu.CompilerParams(
            dimension_semantics=("arbitrary", "arbitrary"), vmem_limit_bytes=VMEM_LIMIT),
        name="retrieve",
    )(h2d, gffn, wq, keys)


def _gelu(s):
    return 0.5 * s * (1.0 + lax.erf(s * math.sqrt(0.5)))


def _experts_kernel(xfb_ref, u_ref, vt_ref, s1_ref, a_ref, s2_ref, e2_ref, tau_ref, h_ref,
                    gfin_ref, out_ref, acc_ref, st_ref, wt_ref):
    e = pl.program_id(1)
    tm = xfb_ref.shape[0]
    rows_per_step = EXP_TILE // N_KEYS
    rows_per_chunk = EXP_CHUNK // N_KEYS

    @pl.when(e == 0)
    def _():
        acc_ref[...] = jnp.zeros_like(acc_ref)

    nt = (((1,), (1,)), ((), ()))
    for c in range(EXP_TILE // EXP_CHUNK):
        st_ref[...] = lax.dot_general(u_ref[c * EXP_CHUNK:(c + 1) * EXP_CHUNK, :], xfb_ref[...], nt,
                                      preferred_element_type=F32)

        def blk_body(blk, carry):
            k2 = pl.multiple_of(blk * ROW_BLK, ROW_BLK)
            for lt in range(tm // LANES):
                ls = slice(lt * LANES, (lt + 1) * LANES)
                s2v = [s2_ref[hd, pl.ds(k2, ROW_BLK), ls] for hd in range(N_HEADS)]
                e2v = [e2_ref[hd, pl.ds(k2, ROW_BLK), ls] for hd in range(N_HEADS)]
                tauv = [tau_ref[hd, :, ls] for hd in range(N_HEADS)]
                for il in range(rows_per_chunk):
                    k1 = c * rows_per_chunk + il
                    g = jnp.zeros((ROW_BLK, LANES), F32)
                    for hd in range(N_HEADS):
                        s1r = s1_ref[hd, k1:k1 + 1, ls]
                        ar = a_ref[hd, k1:k1 + 1, ls]
                        sel = (s2v[hd] + s1r) >= tauv[hd]
                        g = g + jnp.where(sel, e2v[hd] * ar, 0.0)
                    r0 = pl.multiple_of(il * N_KEYS + k2, ROW_BLK)
                    s = st_ref[pl.ds(r0, ROW_BLK), ls]
                    wt_ref[pl.ds(r0, ROW_BLK), ls] = (g * _gelu(s)).astype(BF16)
            return carry

        lax.fori_loop(0, N_KEYS // ROW_BLK, blk_body, 0)
        acc_ref[...] += jnp.dot(vt_ref[:, c * EXP_CHUNK:(c + 1) * EXP_CHUNK], wt_ref[...],
                                preferred_element_type=F32)

    @pl.when(e == pl.num_programs(1) - 1)
    def _():
        hh = h_ref[...] + acc_ref[...].T
        out_ref[...] = _rmsnorm(hh, gfin_ref[...])


def _experts(xfb, u_bf, vt_bf, s1, a, s2, e2, tau, h2d, gfin):
    t, d = h2d.shape
    n_exp = u_bf.shape[0]
    n_tok_tiles = t // TOK_TILE_E
    n_exp_tiles = n_exp // EXP_TILE
    rows = EXP_TILE // N_KEYS
    first_key = lambda: pl.BlockSpec((N_HEADS, rows, TOK_TILE_E), lambda i, e: (0, e, i))
    second_key = lambda: pl.BlockSpec((N_HEADS, N_KEYS, TOK_TILE_E), lambda i, e: (0, 0, i))
    return pl.pallas_call(
        _experts_kernel,
        grid=(n_tok_tiles, n_exp_tiles),
        in_specs=[
            pl.BlockSpec((TOK_TILE_E, d), lambda i, e: (i, 0)),
            pl.BlockSpec((EXP_TILE, d), lambda i, e: (e, 0)),
            pl.BlockSpec((d, EXP_TILE), lambda i, e: (0, e)),
            first_key(), first_key(), second_key(), second_key(),
            pl.BlockSpec((N_HEADS, 1, TOK_TILE_E), lambda i, e: (0, 0, i)),
            pl.BlockSpec((TOK_TILE_E, d), lambda i, e: (i, 0)),
            pl.BlockSpec((1, d), lambda i, e: (0, 0)),
        ],
        out_specs=pl.BlockSpec((TOK_TILE_E, d), lambda i, e: (i, 0)),
        out_shape=jax.ShapeDtypeStruct((t, d), F32),
        scratch_shapes=[pltpu.VMEM((d, TOK_TILE_E), F32),
                        pltpu.VMEM((EXP_CHUNK, TOK_TILE_E), F32),
                        pltpu.VMEM((EXP_CHUNK, TOK_TILE_E), BF16)],
        compiler_params=pltpu.CompilerParams(
            dimension_semantics=("arbitrary", "arbitrary"), vmem_limit_bytes=VMEM_LIMIT),
        name="experts",
    )(xfb, u_bf, vt_bf, s1, a, s2, e2, tau, h2d, gfin)


def kernel(x, meta_tokens, norm_mix_g, w_in, pool_w, pool_scale, conv_w, norm_a_g, norm_b_g, w_out,
           norm_ffn_g, w_query, sub_keys, expert_u, expert_v, norm_final_g):
    b, s, d = x.shape
    assert w_in.shape[0] == 1, "single layer"
    assert s % SEQ_TILE == 0 and (b * s) % TOK_TILE_R == 0 and (b * s) % TOK_TILE_E == 0
    row = lambda v: v.reshape(1, -1)

    h = _mixer(x, meta_tokens, row(norm_mix_g[0]), w_in[0].astype(BF16), pool_w[0].astype(BF16),
               row(pool_scale[0]), conv_w[0], row(norm_a_g[0]), row(norm_b_g[0]),
               w_out[0].astype(BF16))
    h2d = h.reshape(b * s, d)

    xfb, s1, a, s2, e2, tau = _retrieve(h2d, row(norm_ffn_g[0]), w_query[0].astype(BF16),
                                        sub_keys[0].astype(BF16))

    u_bf = expert_u[0].astype(BF16)
    vt_bf = expert_v[0].astype(BF16).T
    out = _experts(xfb, u_bf, vt_bf, s1, a, s2, e2, tau, h2d, row(norm_final_g))
    return out.reshape(b, s, d)
```

```python
import functools
import math

import jax
import jax.numpy as jnp
from jax import lax
from jax.experimental import pallas as pl
from jax.experimental.pallas import tpu as pltpu

F32 = jnp.float32
BF16 = jnp.bfloat16

D_MODEL = 1024
N_META = 16
MIX_A = 512
MIX_B = 512
N_POOL_GROUPS = 4
POOL_GROUP_DIM = 128
POOL_WINDOWS = (2, 4, 8, 16)
CONV_WIDTH = 3
N_HEADS = 8
N_KEYS = 128
D_QUERY_HALF = 128
TOPK = 16
EPS = 1e-6

LANES = 128
SUBLANES = 8
HIST_A = 16
HIST_U = 8

SEQ_TILE = 512
TOK_TILE_R = 512
TOK_TILE_E = 512
EXP_TILE = 2048
HALF_TILES = 2
SCORE_PIECE = 512
OUT_PIECES = 4

VMEM_LIMIT = 56 * 1024 * 1024


def _rmsnorm(x, g):
    ms = jnp.mean(x * x, axis=-1, keepdims=True)
    return (x * lax.rsqrt(ms + EPS)) * g


def _mixer_kernel(x_ref, meta_ref, gmix_ref, win_ref, poolw_ref, pscale_ref, convw_ref,
                  ga_ref, gb_ref, wout_ref, h_ref, abuf, ubuf):
    j = pl.program_id(1)
    ts = x_ref.shape[1]

    def project(rows):
        xn = _rmsnorm(rows, gmix_ref[...])
        return jnp.dot(xn.astype(BF16), win_ref[...], preferred_element_type=F32)

    @pl.when(j == 0)
    def _():
        pm = project(meta_ref[...])
        abuf[0:HIST_A, :] = pm[:, 0:MIX_A]
        um = pm[:, MIX_A + MIX_B:MIX_A + 2 * MIX_B] * pm[:, MIX_A + 2 * MIX_B:]
        ubuf[0:HIST_U, :] = um[N_META - HIST_U:, :]

    @pl.when(j > 0)
    def _():
        abuf[0:HIST_A, :] = abuf[ts:ts + HIST_A, :]
        ubuf[0:HIST_U, :] = ubuf[ts:ts + HIST_U, :]

    x = x_ref[0]
    p = project(x)
    abuf[HIST_A:, :] = p[:, 0:MIX_A]
    ubuf[HIST_U:, :] = p[:, MIX_A + MIX_B:MIX_A + 2 * MIX_B] * p[:, MIX_A + 2 * MIX_B:]
    b_gate = p[:, MIX_A:MIX_A + MIX_B]

    ya_parts = []
    for g in range(N_POOL_GROUPS):
        w = POOL_WINDOWS[g]
        cols = slice(g * POOL_GROUP_DIM, (g + 1) * POOL_GROUP_DIM)
        a_self = abuf[HIST_A:HIST_A + ts, cols]
        acc = a_self
        for k in range(1, w):
            acc = acc + abuf[HIST_A - k:HIST_A - k + ts, cols]
        pooled = acc * (1.0 / w) - a_self
        y = jnp.dot(pooled.astype(BF16), poolw_ref[g], preferred_element_type=F32)
        ya_parts.append(y * pscale_ref[:, cols])
    ya = _rmsnorm(jnp.concatenate(ya_parts, axis=-1), ga_ref[...])

    conv = ubuf[HIST_U - 2:HIST_U - 2 + ts, :] * convw_ref[0:1, :]
    conv = conv + ubuf[HIST_U - 1:HIST_U - 1 + ts, :] * convw_ref[1:2, :]
    conv = conv + ubuf[HIST_U:HIST_U + ts, :] * convw_ref[2:3, :]
    yb = _rmsnorm(b_gate * conv, gb_ref[...])

    mixed = jnp.dot(ya.astype(BF16), wout_ref[0:MIX_A, :], preferred_element_type=F32)
    mixed = mixed + jnp.dot(yb.astype(BF16), wout_ref[MIX_A:, :], preferred_element_type=F32)
    h_ref[0] = x + mixed


def _mixer(x, meta, gmix, win, poolw, pscale, convw, ga, gb, wout):
    b, s, d = x.shape
    n_tiles = s // SEQ_TILE
    const = lambda *shape: pl.BlockSpec(shape, lambda bi, j: (0,) * len(shape))
    return pl.pallas_call(
        _mixer_kernel,
        grid=(b, n_tiles),
        in_specs=[
            pl.BlockSpec((1, SEQ_TILE, d), lambda bi, j: (bi, j, 0)),
            const(N_META, d), const(1, d), const(d, MIX_A + 3 * MIX_B),
            const(N_POOL_GROUPS, POOL_GROUP_DIM, POOL_GROUP_DIM), const(1, MIX_A),
            const(CONV_WIDTH, MIX_B), const(1, MIX_A), const(1, MIX_B), const(MIX_A + MIX_B, d),
        ],
        out_specs=pl.BlockSpec((1, SEQ_TILE, d), lambda bi, j: (bi, j, 0)),
        out_shape=jax.ShapeDtypeStruct((b, s, d), F32),
        scratch_shapes=[pltpu.VMEM((SEQ_TILE + HIST_A, MIX_A), F32),
                        pltpu.VMEM((SEQ_TILE + HIST_U, MIX_B), F32)],
        compiler_params=pltpu.CompilerParams(
            dimension_semantics=("arbitrary", "arbitrary"), vmem_limit_bytes=VMEM_LIMIT),
        name="mixer",
    )(x, meta, gmix, win, poolw, pscale, convw, ga, gb, wout)


def _odd_even_merge_sort_pairs(n):
    pairs = []

    def merge(lo, m, r):
        step = 2 * r
        if step < m:
            merge(lo, m, step)
            merge(lo + r, m, step)
            pairs.extend((i, i + r) for i in range(lo + r, lo + m - r, step))
        else:
            pairs.append((lo, lo + r))

    def sort(lo, m):
        if m > 1:
            sort(lo, m // 2)
            sort(lo + m // 2, m // 2)
            merge(lo, m, 1)

    sort(0, n)
    return pairs


_SORT_PAIRS = _odd_even_merge_sort_pairs(TOPK)


def _compare_exchange(v, i, j):
    v[i], v[j] = jnp.maximum(v[i], v[j]), jnp.minimum(v[i], v[j])


def _sort_desc(v):
    v = list(v)
    for i, j in _SORT_PAIRS:
        _compare_exchange(v, i, j)
    return v


def _bitonic_sort_desc(v):
    v = list(v)
    d = TOPK // 2
    while d >= 1:
        for i in range(TOPK):
            if i & d == 0:
                _compare_exchange(v, i, i + d)
        d //= 2
    return v


def _merge_sublanes(v):
    for shift in (4, 2, 1):
        p = [pltpu.roll(x, shift, 0) for x in v]
        v = _bitonic_sort_desc([jnp.maximum(v[i], p[TOPK - 1 - i]) for i in range(TOPK)])
    return v


def _top16(cols):
    return _merge_sublanes(_sort_desc(cols))


def _retrieve_kernel(h_ref, gffn_ref, wq_ref, keys_ref, xfb_ref, th_ref, a_ref, s2_ref, e2_ref,
                     s1_scr):
    tm = h_ref.shape[0]
    lane_tiles = tm // LANES
    d_query = 2 * D_QUERY_HALF

    xfb = _rmsnorm(h_ref[...], gffn_ref[...]).astype(BF16)
    xfb_ref[...] = xfb
    q_all = jnp.dot(xfb, wq_ref[...], preferred_element_type=F32).astype(BF16)
    nt = (((1,), (1,)), ((), ()))
    for k in range(N_HEADS):
        q = q_all[:, k * d_query:(k + 1) * d_query]
        s1 = lax.dot_general(keys_ref[k, 0], q[:, 0:D_QUERY_HALF], nt, preferred_element_type=F32)
        s2 = lax.dot_general(keys_ref[k, 1], q[:, D_QUERY_HALF:], nt, preferred_element_type=F32)
        for lt in range(lane_tiles):
            s1_scr[k, lt] = s1[:, lt * LANES:(lt + 1) * LANES]
            s2_ref[lt, k] = s2[:, lt * LANES:(lt + 1) * LANES]

    half = TOPK // 2
    n_pieces = N_KEYS // SUBLANES

    def lane_tile_body(idx, carry):
        hd = lax.div(idx, lane_tiles)
        lt = lax.rem(idx, lane_tiles)
        sub = lax.broadcasted_iota(jnp.int32, (SUBLANES, LANES), 0)
        inf = jnp.full((SUBLANES, LANES), jnp.inf, F32)
        x1 = [s1_scr[hd, lt, v * SUBLANES:(v + 1) * SUBLANES, :] for v in range(n_pieces)]
        x2 = [s2_ref[lt, hd, v * SUBLANES:(v + 1) * SUBLANES, :] for v in range(n_pieces)]
        t1 = _top16(x1)
        t2 = _top16(x2)
        t2_lo, t2_hi = t2[half - 1], t2[TOPK - 1]
        for j in range(half - 2, -1, -1):
            t2_lo = jnp.where(sub == j, t2[j], t2_lo)
            t2_hi = jnp.where(sub == j, t2[half + j], t2_hi)
        col = [t1[i] + t2_lo for i in range(TOPK)]
        extra = t1[0] + t2_hi
        col = [col[0]] + [jnp.maximum(col[i], jnp.minimum(col[i - 1], extra)) for i in range(1, TOPK)]
        cs = _merge_sublanes(col)
        z = jnp.ones_like(cs[0])
        for c in cs[1:]:
            z = z + jnp.exp(c - cs[0])
        tau = cs[TOPK - 1]
        th = [inf] * n_pieces
        for j in range(half):
            th = [jnp.where((x1[v] + t2[j]) >= tau, t2[j], th[v]) for v in range(n_pieces)]
        th_top = inf
        for j in range(half, TOPK):
            th_top = jnp.where((t1[0] + t2[j]) >= tau, t2[j], th_top)
        th = [jnp.where(x1[v] == t1[0], jnp.minimum(th[v], th_top), th[v]) for v in range(n_pieces)]
        scale = 0.5 / z
        for v in range(n_pieces):
            rows = slice(v * SUBLANES, (v + 1) * SUBLANES)
            th_ref[lt, hd, rows, :] = th[v]
            a_ref[lt, hd, rows, :] = jnp.exp(x1[v] - t1[0]) * scale
            e2_ref[lt, hd, rows, :] = jnp.exp(x2[v] - t2[0])
        return carry

    lax.fori_loop(0, N_HEADS * lane_tiles, lane_tile_body, 0)


def _retrieve(h2d, gffn, wq, keys):
    t, d = h2d.shape
    n_tiles = t // TOK_TILE_R
    lane_tiles = TOK_TILE_R // LANES
    stats = lambda: pl.BlockSpec((lane_tiles, N_HEADS, N_KEYS, LANES), lambda i: (i, 0, 0, 0))
    stat = jax.ShapeDtypeStruct((t // LANES, N_HEADS, N_KEYS, LANES), F32)
    return pl.pallas_call(
        _retrieve_kernel,
        grid=(n_tiles,),
        in_specs=[
            pl.BlockSpec((TOK_TILE_R, d), lambda i: (i, 0)),
            pl.BlockSpec((1, d), lambda i: (0, 0)),
            pl.BlockSpec((d, N_HEADS * 2 * D_QUERY_HALF), lambda i: (0, 0)),
            pl.BlockSpec((N_HEADS, 2, N_KEYS, D_QUERY_HALF), lambda i: (0, 0, 0, 0)),
        ],
        out_specs=[
            pl.BlockSpec((TOK_TILE_R, d), lambda i: (i, 0)),
            stats(), stats(), stats(), stats(),
        ],
        out_shape=[jax.ShapeDtypeStruct((t, d), BF16), stat, stat, stat, stat],
        scratch_shapes=[pltpu.VMEM((N_HEADS, lane_tiles, N_KEYS, LANES), F32)],
        compiler_params=pltpu.CompilerParams(
            dimension_semantics=("arbitrary",), vmem_limit_bytes=VMEM_LIMIT),
        name="retrieve",
    )(h2d, gffn, wq, keys)


def _order_after(x):
    return jnp.where(x != x, 1.0, 0.0)


def _experts_kernel(xfb_ref, u_ref, vt_ref, vtp_ref, th_ref, a_ref, s2_ref, e2_ref, h_ref, gfin_ref,
                    out_ref, acc_ref, wt_ref):
    e = pl.program_id(1)
    half_tokens = HALF_TILES * LANES
    pieces = EXP_TILE // SCORE_PIECE
    rows_per_piece = SCORE_PIECE // N_KEYS
    blocks_per_half = (EXP_TILE // N_KEYS) * HALF_TILES
    out_rows = D_MODEL // OUT_PIECES
    nt = (((1,), (1,)), ((), ()))
    sqrt_half = math.sqrt(0.5)

    @pl.when(e == 0)
    def _():
        acc_ref[...] = jnp.zeros_like(acc_ref)
        for lt in range(HALF_TILES, 2 * HALF_TILES):
            wt_ref[lt] = jnp.zeros(wt_ref.shape[1:], BF16)

    def accumulate(v_ref, half, rows):
        w = jnp.concatenate([wt_ref[half * HALF_TILES + l] for l in range(HALF_TILES)], axis=1)
        new = acc_ref[half, rows, :] + jnp.dot(v_ref[rows, :], w, preferred_element_type=F32)
        acc_ref[half, rows, :] = new
        return new

    for half in range(2):
        x_half = xfb_ref[half * half_tokens:(half + 1) * half_tokens, :]
        prev_ref, prev_half = (vtp_ref, 1) if half == 0 else (vt_ref, 0)
        waits = {}
        for p in range(OUT_PIECES):
            new = accumulate(prev_ref, prev_half, slice(p * out_rows, (p + 1) * out_rows))
            at_block = (2 * p + 1) * blocks_per_half // (2 * OUT_PIECES)
            waits[at_block] = _order_after(new[0:1, 0:LANES])
        chain = jnp.zeros((1, LANES), F32)
        block = 0
        for p in range(pieces):
            s_piece = lax.dot_general(u_ref[p * SCORE_PIECE:(p + 1) * SCORE_PIECE, :], x_half, nt,
                                      preferred_element_type=F32)
            for l in range(HALF_TILES):
                lt = half * HALF_TILES + l
                for i in range(rows_per_piece):
                    il = p * rows_per_piece + i
                    if block in waits:
                        chain = chain + waits[block]
                    g = None
                    for hd in range(N_HEADS):
                        thr = th_ref[lt, hd, il:il + 1, :]
                        if hd == 0:
                            thr = thr + chain
                        ar = a_ref[lt, hd, il:il + 1, :]
                        term = jnp.where(s2_ref[lt, hd] >= thr, e2_ref[lt, hd] * ar, 0.0)
                        g = term if g is None else g + term
                    s = s_piece[i * N_KEYS:(i + 1) * N_KEYS, l * LANES:(l + 1) * LANES]
                    w = g * s
                    wt_ref[lt, il * N_KEYS:(il + 1) * N_KEYS, :] = (
                        w + w * lax.erf(s * sqrt_half)).astype(BF16)
                    chain = _order_after(g[0:1, :])
                    block += 1

    @pl.when(e == pl.num_programs(1) - 1)
    def _():
        accumulate(vt_ref, 1, slice(0, D_MODEL))
        acc_t = jnp.concatenate([acc_ref[0], acc_ref[1]], axis=1)
        out_ref[...] = _rmsnorm(h_ref[...] + acc_t.T, gfin_ref[...])


def _experts(xfb, u_bf, vt_bf, th, a, s2, e2, h2d, gfin):
    t, d = h2d.shape
    n_exp = u_bf.shape[0]
    n_tok_tiles = t // TOK_TILE_E
    n_exp_tiles = n_exp // EXP_TILE
    lane_tiles = TOK_TILE_E // LANES
    rows = EXP_TILE // N_KEYS
    first_key = lambda: pl.BlockSpec((lane_tiles, N_HEADS, rows, LANES), lambda i, e: (i, 0, e, 0))
    second_key = lambda: pl.BlockSpec((lane_tiles, N_HEADS, N_KEYS, LANES), lambda i, e: (i, 0, 0, 0))
    return pl.pallas_call(
        _experts_kernel,
        grid=(n_tok_tiles, n_exp_tiles),
        in_specs=[
            pl.BlockSpec((TOK_TILE_E, d), lambda i, e: (i, 0)),
            pl.BlockSpec((EXP_TILE, d), lambda i, e: (e, 0)),
            pl.BlockSpec((d, EXP_TILE), lambda i, e: (0, e)),
            pl.BlockSpec((d, EXP_TILE), lambda i, e: (0, jnp.maximum(e - 1, 0))),
            first_key(), first_key(), second_key(), second_key(),
            pl.BlockSpec((TOK_TILE_E, d), lambda i, e: (i, 0)),
            pl.BlockSpec((1, d), lambda i, e: (0, 0)),
        ],
        out_specs=pl.BlockSpec((TOK_TILE_E, d), lambda i, e: (i, 0)),
        out_shape=jax.ShapeDtypeStruct((t, d), F32),
        scratch_shapes=[pltpu.VMEM((2, d, HALF_TILES * LANES), F32),
                        pltpu.VMEM((lane_tiles, EXP_TILE, LANES), BF16)],
        compiler_params=pltpu.CompilerParams(
            dimension_semantics=("arbitrary", "arbitrary"), vmem_limit_bytes=VMEM_LIMIT),
        name="experts",
    )(xfb, u_bf, vt_bf, vt_bf, th, a, s2, e2, h2d, gfin)


def kernel(x, meta_tokens, norm_mix_g, w_in, pool_w, pool_scale, conv_w, norm_a_g, norm_b_g, w_out,
           norm_ffn_g, w_query, sub_keys, expert_u, expert_v, norm_final_g):
    b, s, d = x.shape
    assert w_in.shape[0] == 1, "single layer"
    assert s % SEQ_TILE == 0 and (b * s) % TOK_TILE_R == 0 and (b * s) % TOK_TILE_E == 0
    row = lambda v: v.reshape(1, -1)

    h = _mixer(x, meta_tokens, row(norm_mix_g[0]), w_in[0].astype(BF16), pool_w[0].astype(BF16),
               row(pool_scale[0]), conv_w[0], row(norm_a_g[0]), row(norm_b_g[0]),
               w_out[0].astype(BF16))
    h2d = h.reshape(b * s, d)

    xfb, th, a, s2, e2 = _retrieve(h2d, row(norm_ffn_g[0]), w_query[0].astype(BF16),
                                   sub_keys[0].astype(BF16))

    u_bf = expert_u[0].astype(BF16)
    vt_bf = expert_v[0].astype(BF16).T
    out = _experts(xfb, u_bf, vt_bf, th, a, s2, e2, h2d, row(norm_final_g))
    return out.reshape(b, s, d)
```

```python
import math

import jax
import jax.numpy as jnp
from jax import lax
from jax.experimental import pallas as pl
from jax.experimental.pallas import tpu as pltpu

F32 = jnp.float32
BF16 = jnp.bfloat16

D_MODEL = 1024
N_META = 16
MIX_A = 512
MIX_B = 512
N_POOL_GROUPS = 4
POOL_GROUP_DIM = 128
POOL_WINDOWS = (2, 4, 8, 16)
CONV_WIDTH = 3
N_HEADS = 8
N_KEYS = 128
D_QUERY_HALF = 128
TOPK = 16
EPS = 1e-6

LANES = 128
SUBLANES = 8
HIST_A = 16
HIST_U = 8

SEQ_TILE = 512
TOK_TILE_R = 512
TOK_TILE_E = 512
EXP_TILE = 2048
HALF_TILES = 2
SCORE_PIECE = 512
OUT_PIECES = 4

VMEM_LIMIT = 56 * 1024 * 1024


def _rmsnorm(x, g):
    ms = jnp.mean(x * x, axis=-1, keepdims=True)
    return (x * lax.rsqrt(ms + EPS)) * g


def _mixer_kernel(x_ref, meta_ref, gmix_ref, win_ref, poolw_ref, pscale_ref, convw_ref,
                  ga_ref, gb_ref, wout_ref, h_ref, abuf, ubuf):
    j = pl.program_id(1)
    ts = x_ref.shape[1]

    def project(rows):
        xn = _rmsnorm(rows, gmix_ref[...])
        return jnp.dot(xn.astype(BF16), win_ref[...], preferred_element_type=F32)

    @pl.when(j == 0)
    def _():
        pm = project(meta_ref[...])
        abuf[0:HIST_A, :] = pm[:, 0:MIX_A]
        um = pm[:, MIX_A + MIX_B:MIX_A + 2 * MIX_B] * pm[:, MIX_A + 2 * MIX_B:]
        ubuf[0:HIST_U, :] = um[N_META - HIST_U:, :]

    @pl.when(j > 0)
    def _():
        abuf[0:HIST_A, :] = abuf[ts:ts + HIST_A, :]
        ubuf[0:HIST_U, :] = ubuf[ts:ts + HIST_U, :]

    x = x_ref[0]
    p = project(x)
    abuf[HIST_A:, :] = p[:, 0:MIX_A]
    ubuf[HIST_U:, :] = p[:, MIX_A + MIX_B:MIX_A + 2 * MIX_B] * p[:, MIX_A + 2 * MIX_B:]
    b_gate = p[:, MIX_A:MIX_A + MIX_B]

    ya_parts = []
    for g in range(N_POOL_GROUPS):
        w = POOL_WINDOWS[g]
        cols = slice(g * POOL_GROUP_DIM, (g + 1) * POOL_GROUP_DIM)
        a_self = abuf[HIST_A:HIST_A + ts, cols]
        acc = a_self
        for k in range(1, w):
            acc = acc + abuf[HIST_A - k:HIST_A - k + ts, cols]
        pooled = acc * (1.0 / w) - a_self
        y = jnp.dot(pooled.astype(BF16), poolw_ref[g], preferred_element_type=F32)
        ya_parts.append(y * pscale_ref[:, cols])
    ya = _rmsnorm(jnp.concatenate(ya_parts, axis=-1), ga_ref[...])

    conv = ubuf[HIST_U - 2:HIST_U - 2 + ts, :] * convw_ref[0:1, :]
    conv = conv + ubuf[HIST_U - 1:HIST_U - 1 + ts, :] * convw_ref[1:2, :]
    conv = conv + ubuf[HIST_U:HIST_U + ts, :] * convw_ref[2:3, :]
    yb = _rmsnorm(b_gate * conv, gb_ref[...])

    mixed = jnp.dot(ya.astype(BF16), wout_ref[0:MIX_A, :], preferred_element_type=F32)
    mixed = mixed + jnp.dot(yb.astype(BF16), wout_ref[MIX_A:, :], preferred_element_type=F32)
    h_ref[0] = x + mixed


def _mixer(x, meta, gmix, win, poolw, pscale, convw, ga, gb, wout):
    b, s, d = x.shape
    n_tiles = s // SEQ_TILE
    const = lambda *shape: pl.BlockSpec(shape, lambda bi, j: (0,) * len(shape))
    return pl.pallas_call(
        _mixer_kernel,
        grid=(b, n_tiles),
        in_specs=[
            pl.BlockSpec((1, SEQ_TILE, d), lambda bi, j: (bi, j, 0)),
            const(N_META, d), const(1, d), const(d, MIX_A + 3 * MIX_B),
            const(N_POOL_GROUPS, POOL_GROUP_DIM, POOL_GROUP_DIM), const(1, MIX_A),
            const(CONV_WIDTH, MIX_B), const(1, MIX_A), const(1, MIX_B), const(MIX_A + MIX_B, d),
        ],
        out_specs=pl.BlockSpec((1, SEQ_TILE, d), lambda bi, j: (bi, j, 0)),
        out_shape=jax.ShapeDtypeStruct((b, s, d), F32),
        scratch_shapes=[pltpu.VMEM((SEQ_TILE + HIST_A, MIX_A), F32),
                        pltpu.VMEM((SEQ_TILE + HIST_U, MIX_B), F32)],
        compiler_params=pltpu.CompilerParams(
            dimension_semantics=("arbitrary", "arbitrary"), vmem_limit_bytes=VMEM_LIMIT),
        name="mixer",
    )(x, meta, gmix, win, poolw, pscale, convw, ga, gb, wout)


def _odd_even_merge_sort_pairs(n):
    pairs = []

    def merge(lo, m, r):
        step = 2 * r
        if step < m:
            merge(lo, m, step)
            merge(lo + r, m, step)
            pairs.extend((i, i + r) for i in range(lo + r, lo + m - r, step))
        else:
            pairs.append((lo, lo + r))

    def sort(lo, m):
        if m > 1:
            sort(lo, m // 2)
            sort(lo + m // 2, m // 2)
            merge(lo, m, 1)

    sort(0, n)
    return pairs


_SORT_PAIRS = _odd_even_merge_sort_pairs(TOPK)


def _compare_exchange(v, i, j):
    v[i], v[j] = jnp.maximum(v[i], v[j]), jnp.minimum(v[i], v[j])


def _sort_desc(v):
    v = list(v)
    for i, j in _SORT_PAIRS:
        _compare_exchange(v, i, j)
    return v


def _bitonic_sort_desc(v):
    v = list(v)
    d = TOPK // 2
    while d >= 1:
        for i in range(TOPK):
            if i & d == 0:
                _compare_exchange(v, i, i + d)
        d //= 2
    return v


def _merge_sublanes(v, sort_result=True):
    for shift in (4, 2, 1):
        p = [pltpu.roll(x, shift, 0) for x in v]
        v = [jnp.maximum(v[i], p[TOPK - 1 - i]) for i in range(TOPK)]
        if sort_result or shift != 1:
            v = _bitonic_sort_desc(v)
    return v


def _top16(cols):
    return _merge_sublanes(_sort_desc(cols))


def _retrieve_kernel(h_ref, gffn_ref, wq_ref, keys_ref, xfb_ref, th_ref, a_ref, s2_ref, e2_ref,
                     s1_scr):
    tm = h_ref.shape[0]
    lane_tiles = tm // LANES
    d_query = 2 * D_QUERY_HALF

    xfb = _rmsnorm(h_ref[...], gffn_ref[...]).astype(BF16)
    xfb_ref[...] = xfb
    q_all = jnp.dot(xfb, wq_ref[...], preferred_element_type=F32).astype(BF16)
    nt = (((1,), (1,)), ((), ()))
    for k in range(N_HEADS):
        q = q_all[:, k * d_query:(k + 1) * d_query]
        s1 = lax.dot_general(keys_ref[k, 0], q[:, 0:D_QUERY_HALF], nt, preferred_element_type=F32)
        s2 = lax.dot_general(keys_ref[k, 1], q[:, D_QUERY_HALF:], nt, preferred_element_type=F32)
        for lt in range(lane_tiles):
            s1_scr[k, lt] = s1[:, lt * LANES:(lt + 1) * LANES]
            s2_ref[lt, k] = s2[:, lt * LANES:(lt + 1) * LANES]

    half = TOPK // 2
    n_pieces = N_KEYS // SUBLANES

    def lane_tile_body(idx, carry):
        hd = lax.div(idx, lane_tiles)
        lt = lax.rem(idx, lane_tiles)
        sub = lax.broadcasted_iota(jnp.int32, (SUBLANES, LANES), 0)
        inf = jnp.full((SUBLANES, LANES), jnp.inf, F32)
        x1 = [s1_scr[hd, lt, v * SUBLANES:(v + 1) * SUBLANES, :] for v in range(n_pieces)]
        x2 = [s2_ref[lt, hd, v * SUBLANES:(v + 1) * SUBLANES, :] for v in range(n_pieces)]
        t1 = _top16(x1)
        t2 = _top16(x2)
        t2_lo, t2_hi = t2[half - 1], t2[TOPK - 1]
        for j in range(half - 2, -1, -1):
            t2_lo = jnp.where(sub == j, t2[j], t2_lo)
            t2_hi = jnp.where(sub == j, t2[half + j], t2_hi)
        col = [t1[i] + t2_lo for i in range(TOPK)]
        extra = t1[0] + t2_hi
        col = [col[0]] + [jnp.maximum(col[i], jnp.minimum(col[i - 1], extra)) for i in range(1, TOPK)]
        cs = _merge_sublanes(col, sort_result=False)
        top, tau = cs[0], cs[0]
        for c in cs[1:]:
            top, tau = jnp.maximum(top, c), jnp.minimum(tau, c)
        z = jnp.exp(cs[0] - top)
        for c in cs[1:]:
            z = z + jnp.exp(c - top)
        th = [inf] * n_pieces
        for j in range(half):
            th = [jnp.where((x1[v] + t2[j]) >= tau, t2[j], th[v]) for v in range(n_pieces)]
        th_top = inf
        for j in range(half, TOPK):
            th_top = jnp.where((t1[0] + t2[j]) >= tau, t2[j], th_top)
        th = [jnp.where(x1[v] == t1[0], jnp.minimum(th[v], th_top), th[v]) for v in range(n_pieces)]
        scale = 0.5 / z
        for v in range(n_pieces):
            rows = slice(v * SUBLANES, (v + 1) * SUBLANES)
            th_ref[lt, hd, rows, :] = th[v]
            a_ref[lt, hd, rows, :] = jnp.exp(x1[v] - t1[0]) * scale
            e2_ref[lt, hd, rows, :] = jnp.exp(x2[v] - t2[0])
        return carry

    lax.fori_loop(0, N_HEADS * lane_tiles, lane_tile_body, 0)


def _retrieve(h2d, gffn, wq, keys):
    t, d = h2d.shape
    n_tiles = t // TOK_TILE_R
    lane_tiles = TOK_TILE_R // LANES
    stats = lambda: pl.BlockSpec((lane_tiles, N_HEADS, N_KEYS, LANES), lambda i: (i, 0, 0, 0))
    stat = jax.ShapeDtypeStruct((t // LANES, N_HEADS, N_KEYS, LANES), F32)
    return pl.pallas_call(
        _retrieve_kernel,
        grid=(n_tiles,),
        in_specs=[
            pl.BlockSpec((TOK_TILE_R, d), lambda i: (i, 0)),
            pl.BlockSpec((1, d), lambda i: (0, 0)),
            pl.BlockSpec((d, N_HEADS * 2 * D_QUERY_HALF), lambda i: (0, 0)),
            pl.BlockSpec((N_HEADS, 2, N_KEYS, D_QUERY_HALF), lambda i: (0, 0, 0, 0)),
        ],
        out_specs=[
            pl.BlockSpec((TOK_TILE_R, d), lambda i: (i, 0)),
            stats(), stats(), stats(), stats(),
        ],
        out_shape=[jax.ShapeDtypeStruct((t, d), BF16), stat, stat, stat, stat],
        scratch_shapes=[pltpu.VMEM((N_HEADS, lane_tiles, N_KEYS, LANES), F32)],
        compiler_params=pltpu.CompilerParams(
            dimension_semantics=("arbitrary",), vmem_limit_bytes=VMEM_LIMIT),
        name="retrieve",
    )(h2d, gffn, wq, keys)


def _order_after(x):
    return jnp.where(x != x, 1.0, 0.0)


def _experts_kernel(xfb_ref, u_ref, vt_ref, vtp_ref, th_ref, a_ref, s2_ref, e2_ref, h_ref, gfin_ref,
                    out_ref, acc_ref, wt_ref):
    e = pl.program_id(1)
    half_tokens = HALF_TILES * LANES
    pieces = EXP_TILE // SCORE_PIECE
    rows_per_piece = SCORE_PIECE // N_KEYS
    blocks_per_half = (EXP_TILE // N_KEYS) * HALF_TILES
    out_rows = D_MODEL // OUT_PIECES
    nt = (((1,), (1,)), ((), ()))
    sqrt_half = math.sqrt(0.5)

    @pl.when(e == 0)
    def _():
        acc_ref[...] = jnp.zeros_like(acc_ref)
        for lt in range(HALF_TILES, 2 * HALF_TILES):
            wt_ref[lt] = jnp.zeros(wt_ref.shape[1:], BF16)

    def accumulate(v_ref, half, rows):
        w = jnp.concatenate([wt_ref[half * HALF_TILES + l] for l in range(HALF_TILES)], axis=1)
        new = acc_ref[half, rows, :] + jnp.dot(v_ref[rows, :], w, preferred_element_type=F32)
        acc_ref[half, rows, :] = new
        return new

    for half in range(2):
        x_half = xfb_ref[half * half_tokens:(half + 1) * half_tokens, :]
        prev_ref, prev_half = (vtp_ref, 1) if half == 0 else (vt_ref, 0)
        waits = {}
        for p in range(OUT_PIECES):
            new = accumulate(prev_ref, prev_half, slice(p * out_rows, (p + 1) * out_rows))
            at_block = (2 * p + 1) * blocks_per_half // (2 * OUT_PIECES)
            waits[at_block] = _order_after(new[0:1, 0:LANES])
        chain = jnp.zeros((1, LANES), F32)
        block = 0
        for p in range(pieces):
            s_piece = lax.dot_general(u_ref[p * SCORE_PIECE:(p + 1) * SCORE_PIECE, :], x_half, nt,
                                      preferred_element_type=F32)
            for l in range(HALF_TILES):
                lt = half * HALF_TILES + l
                for i in range(rows_per_piece):
                    il = p * rows_per_piece + i
                    if block in waits:
                        chain = chain + waits[block]
                    g = None
                    for hd in range(N_HEADS):
                        thr = th_ref[lt, hd, il:il + 1, :]
                        if hd == 0:
                            thr = thr + chain
                        ar = a_ref[lt, hd, il:il + 1, :]
                        term = jnp.where(s2_ref[lt, hd] >= thr, e2_ref[lt, hd] * ar, 0.0)
                        g = term if g is None else g + term
                    s = s_piece[i * N_KEYS:(i + 1) * N_KEYS, l * LANES:(l + 1) * LANES]
                    w = g * s
                    wt_ref[lt, il * N_KEYS:(il + 1) * N_KEYS, :] = (
                        w + w * lax.erf(s * sqrt_half)).astype(BF16)
                    chain = _order_after(g[0:1, :])
                    block += 1

    @pl.when(e == pl.num_programs(1) - 1)
    def _():
        accumulate(vt_ref, 1, slice(0, D_MODEL))
        acc_t = jnp.concatenate([acc_ref[0], acc_ref[1]], axis=1)
        out_ref[...] = _rmsnorm(h_ref[...] + acc_t.T, gfin_ref[...])


def _experts(xfb, u_bf, vt_bf, th, a, s2, e2, h2d, gfin):
    t, d = h2d.shape
    n_exp = u_bf.shape[0]
    n_tok_tiles = t // TOK_TILE_E
    n_exp_tiles = n_exp // EXP_TILE
    lane_tiles = TOK_TILE_E // LANES
    rows = EXP_TILE // N_KEYS
    first_key = lambda: pl.BlockSpec((lane_tiles, N_HEADS, rows, LANES), lambda i, e: (i, 0, e, 0))
    second_key = lambda: pl.BlockSpec((lane_tiles, N_HEADS, N_KEYS, LANES), lambda i, e: (i, 0, 0, 0))
    return pl.pallas_call(
        _experts_kernel,
        grid=(n_tok_tiles, n_exp_tiles),
        in_specs=[
            pl.BlockSpec((TOK_TILE_E, d), lambda i, e: (i, 0)),
            pl.BlockSpec((EXP_TILE, d), lambda i, e: (e, 0)),
            pl.BlockSpec((d, EXP_TILE), lambda i, e: (0, e)),
            pl.BlockSpec((d, EXP_TILE), lambda i, e: (0, jnp.maximum(e - 1, 0))),
            first_key(), first_key(), second_key(), second_key(),
            pl.BlockSpec((TOK_TILE_E, d), lambda i, e: (i, 0)),
            pl.BlockSpec((1, d), lambda i, e: (0, 0)),
        ],
        out_specs=pl.BlockSpec((TOK_TILE_E, d), lambda i, e: (i, 0)),
        out_shape=jax.ShapeDtypeStruct((t, d), F32),
        scratch_shapes=[pltpu.VMEM((2, d, HALF_TILES * LANES), F32),
                        pltpu.VMEM((lane_tiles, EXP_TILE, LANES), BF16)],
        compiler_params=pltpu.CompilerParams(
            dimension_semantics=("arbitrary", "arbitrary"), vmem_limit_bytes=VMEM_LIMIT),
        name="experts",
    )(xfb, u_bf, vt_bf, vt_bf, th, a, s2, e2, h2d, gfin)


def kernel(x, meta_tokens, norm_mix_g, w_in, pool_w, pool_scale, conv_w, norm_a_g, norm_b_g, w_out,
           norm_ffn_g, w_query, sub_keys, expert_u, expert_v, norm_final_g):
    b, s, d = x.shape
    assert w_in.shape[0] == 1, "single layer"
    assert s % SEQ_TILE == 0 and (b * s) % TOK_TILE_R == 0 and (b * s) % TOK_TILE_E == 0
    row = lambda v: v.reshape(1, -1)

    h = _mixer(x, meta_tokens, row(norm_mix_g[0]), w_in[0].astype(BF16), pool_w[0].astype(BF16),
               row(pool_scale[0]), conv_w[0], row(norm_a_g[0]), row(norm_b_g[0]),
               w_out[0].astype(BF16))
    h2d = h.reshape(b * s, d)

    xfb, th, a, s2, e2 = _retrieve(h2d, row(norm_ffn_g[0]), w_query[0].astype(BF16),
                                   sub_keys[0].astype(BF16))

    u_bf = expert_u[0].astype(BF16)
    vt_bf = expert_v[0].astype(BF16).T
    out = _experts(xfb, u_bf, vt_bf, th, a, s2, e2, h2d, row(norm_final_g))
    return out.reshape(b, s, d)
```
